```python
import math
import jax
import jax.numpy as jnp
from jax import lax
import numpy as np

D_MODEL = 1024
BATCH = 16
SEQ = 2048
DEPTH = 2

CTX_LEN = 256
GRID_W = 64
N_BRANCHES = 4
BRANCH_WIDTH = 512
POOL_WIDTH = 512
POOL_GROUP = 128
POOL_WINDOWS = (2, 4, 8, 16)
SSD_HEADS = 8
SSD_HEADDIM = 64
SSD_INNER = SSD_HEADS * SSD_HEADDIM
SSD_GROUPS = 2
SSD_STATE = 64
SSD_XBC = SSD_INNER + 2 * SSD_GROUPS * SSD_STATE
SSD_CONV = 5
SSD_CHUNK = 128
DIFF_HEADS = 4
DIFF_DK = 64
DIFF_DV = 2 * DIFF_DK
DIFF_QK_WIDTH = DIFF_HEADS * 2 * DIFF_DK
DIFF_V_WIDTH = DIFF_HEADS * DIFF_DV
ATTN_BLOCK = 128
ROPE_BASE = 10000.0
SGU_WIDTH = 512
SGU_GROUPS = 4
SGU_GROUP_W = SGU_WIDTH // SGU_GROUPS
SGU_CHUNK = 128
N_EXPERTS = 16
EXPERT_FF = 512
EC_CAPACITY_FACTOR = 2
LN_EPS = 1e-5
ALPHA = (2 * DEPTH) ** 0.25
BETA = (8 * DEPTH) ** -0.25
PROJ_WIDTHS = (POOL_WIDTH, SSD_INNER, SSD_XBC, 2 * SSD_HEADS, DIFF_QK_WIDTH, DIFF_QK_WIDTH, DIFF_V_WIDTH, 2 * SGU_WIDTH)
IN_COLS = sum(PROJ_WIDTHS)

kernel_name = 'hybrid_parallel_mixer_ec_moe_diffusion'


def layer_norm(x, g, b):
    xf = x.astype(jnp.float32)
    mu = jnp.mean(xf, axis=-1, keepdims=True)
    var = jnp.mean(jnp.square(xf - mu), axis=-1, keepdims=True)
    return ((xf - mu) * lax.rsqrt(var + LN_EPS) * g + b).astype(x.dtype)


def rms_norm(xf, g):
    return xf * lax.rsqrt(jnp.mean(jnp.square(xf), axis=-1, keepdims=True) + LN_EPS) * g


def modulate(x, shift, scale):
    return x * (1 + scale) + shift


def flip(t):
    return jnp.flip(t, axis=1)


def pool_mix(xp, pool_w, pool_scale):
    bsz, n, _ = xp.shape
    xf = xp.astype(jnp.float32)
    cs = jnp.concatenate([jnp.zeros((bsz, 1, POOL_WIDTH), jnp.float32), jnp.cumsum(xf, axis=1)], axis=1)
    t = jnp.arange(n)
    outs = []
    for g, w in enumerate(POOL_WINDOWS):
        lo = jnp.clip(t - w // 2, 0, n)
        hi = jnp.clip(t + w // 2, 0, n)
        csg = cs[:, :, g * POOL_GROUP:(g + 1) * POOL_GROUP]
        mean = (csg[:, hi] - csg[:, lo]) / (hi - lo).astype(jnp.float32)[None, :, None]
        outs.append(mean - xf[:, :, g * POOL_GROUP:(g + 1) * POOL_GROUP])
    pooled = jnp.stack(outs, axis=2).astype(xp.dtype)
    mixed = jnp.einsum('bngc,gcd->bngd', pooled, pool_w).reshape(bsz, n, POOL_WIDTH)
    return mixed * pool_scale


def depthwise_conv(x, w, b):
    ch = x.shape[-1]
    y = lax.conv_general_dilated(x, w[:, None, :].astype(x.dtype), window_strides=(1,),
                                 padding=[(SSD_CONV // 2, SSD_CONV // 2)],
                                 dimension_numbers=('NWC', 'WIO', 'NWC'), feature_group_count=ch)
    return y + b


def ssd_prep(xbc, dt_raw, conv_w, conv_b, dt_bias):
    bsz, n, _ = xbc.shape
    xbc = jax.nn.silu(depthwise_conv(xbc, conv_w, conv_b))
    xs, bm, cm = jnp.split(xbc, [SSD_INNER, SSD_INNER + SSD_GROUPS * SSD_STATE], axis=-1)
    dt = jax.nn.softplus(dt_raw.astype(jnp.float32).reshape(bsz, n, 2, SSD_HEADS) + dt_bias.astype(jnp.float32))
    return (xs.reshape(bsz, n, SSD_HEADS, SSD_HEADDIM),
            bm.reshape(bsz, n, SSD_GROUPS, SSD_STATE),
            cm.reshape(bsz, n, SSD_GROUPS, SSD_STATE),
            dt[:, :, 0], dt[:, :, 1])


def ssd_chunked(x, dt, a_neg, bmat, cmat, init_state, with_output):
    f32 = jnp.float32
    bsz, n, nh, hp = x.shape
    nc = n // SSD_CHUNK
    rep = nh // SSD_GROUPS
    xdt = (x.astype(f32) * dt[..., None]).reshape(bsz, nc, SSD_CHUNK, nh, hp)
    a_cs = jnp.cumsum((dt * a_neg).reshape(bsz, nc, SSD_CHUNK, nh).transpose(0, 3, 1, 2), axis=-1)
    bh = jnp.repeat(bmat.astype(f32), rep, axis=2).reshape(bsz, nc, SSD_CHUNK, nh, SSD_STATE)
    ch = jnp.repeat(cmat.astype(f32), rep, axis=2).reshape(bsz, nc, SSD_CHUNK, nh, SSD_STATE)
    decay_to_end = jnp.exp(a_cs[..., -1:] - a_cs)
    chunk_states = jnp.einsum('bclhn,bhcl,bclhp->bchpn', bh, decay_to_end, xdt)
    chunk_decay = jnp.exp(a_cs[..., -1])

    def carry_step(state, inp):
        s_k, d_k = inp
        return state * d_k[:, :, None, None] + s_k, state

    final, entering = lax.scan(carry_step, init_state,
                               (jnp.moveaxis(chunk_states, 1, 0), jnp.moveaxis(chunk_decay, 2, 0)))
    if not with_output:
        return None, final
    entering = jnp.moveaxis(entering, 0, 1)
    seg = a_cs[..., :, None] - a_cs[..., None, :]
    lower = jnp.tril(jnp.ones((SSD_CHUNK, SSD_CHUNK), dtype=bool))
    lmat = jnp.exp(jnp.where(lower, seg, -jnp.inf))
    scores = jnp.einsum('bclhn,bcshn->bhcls', ch, bh) * lmat
    y = (jnp.einsum('bhcls,bcshp->bclhp', scores, xdt)
         + jnp.einsum('bclhn,bchpn,bhcl->bclhp', ch, entering, jnp.exp(a_cs)))
    return y.reshape(bsz, n, nh, hp), final


def ssd_finish(y_f, y_b, xs, z, d_skip, g):
    bsz, n = z.shape[:2]
    y = y_f + y_b + d_skip.astype(jnp.float32)[:, None] * xs.astype(jnp.float32)
    y = y.reshape(bsz, n, SSD_INNER) * jax.nn.silu(z.astype(jnp.float32))
    return rms_norm(y, g.astype(jnp.float32)).astype(z.dtype)


def rotate(x, pos):
    half = x.shape[-1] // 2
    inv = ROPE_BASE ** (-jnp.arange(half, dtype=jnp.float32) / half)
    ang = pos.astype(jnp.float32)[:, None] * inv
    cos = jnp.cos(ang)[None, :, None, :]
    sin = jnp.sin(ang)[None, :, None, :]
    xf = x.astype(jnp.float32)
    x1, x2 = xf[..., :half], xf[..., half:]
    return jnp.concatenate([x1 * cos - x2 * sin, x1 * sin + x2 * cos], axis=-1).astype(x.dtype)


def axial_rope(x, rows, cols):
    half = x.shape[-1] // 2
    return jnp.concatenate([rotate(x[..., :half], rows), rotate(x[..., half:], cols)], axis=-1)


def diff_heads(q, rows, cols, use_rope):
    bsz, n, _ = q.shape
    q = q.reshape(bsz, n, DIFF_HEADS * 2, DIFF_DK)
    if use_rope:
        q = axial_rope(q, rows, cols)
    q = q.reshape(bsz, n, DIFF_HEADS, 2, DIFF_DK).transpose(0, 2, 3, 1, 4)
    return q[:, :, 0], q[:, :, 1]


def value_heads(v):
    bsz, n, _ = v.shape
    return v.reshape(bsz, n, DIFF_HEADS, DIFF_DV).transpose(0, 2, 1, 3)


def diff_attend(q1, q2, k1, k2, v, lam):
    scale = DIFF_DK ** -0.5
    s1 = jnp.einsum('bhqd,bhkd->bhqk', q1, k1).astype(jnp.float32) * scale
    s2 = jnp.einsum('bhqd,bhkd->bhqk', q2, k2).astype(jnp.float32) * scale
    a = jax.nn.softmax(s1, axis=-1) - lam * jax.nn.softmax(s2, axis=-1)
    return jnp.einsum('bhqk,bhkd->bhqd', a.astype(v.dtype), v)


def blocked_diff_attention(q1, q2, k1, k2, v, lam):
    bsz, nh, n, dk = q1.shape
    nb = n // ATTN_BLOCK

    def blocks(q):
        return q.reshape(bsz, nh, nb, ATTN_BLOCK, dk).transpose(2, 0, 1, 3, 4)

    out = lax.map(lambda qq: diff_attend(qq[0], qq[1], k1, k2, v, lam), (blocks(q1), blocks(q2)))
    return out.transpose(1, 2, 0, 3, 4).reshape(bsz, nh, n, DIFF_DV)


def diff_head_norm(o, g, lam_init):
    bsz, nh, n, dv = o.shape
    of = rms_norm(o.astype(jnp.float32), g.astype(jnp.float32)) * (1.0 - lam_init)
    return of.transpose(0, 2, 1, 3).reshape(bsz, n, nh * dv).astype(o.dtype)


def spatial_gating(uv, ln_g, ln_b, w_s, b_s):
    bsz, n, _ = uv.shape
    u, v = jnp.split(jax.nn.gelu(uv), 2, axis=-1)
    v = layer_norm(v, ln_g, ln_b)
    vg = v.reshape(bsz, n // SGU_CHUNK, SGU_CHUNK, SGU_GROUPS, SGU_GROUP_W)
    vm = jnp.einsum('gpq,bcqgd->bcpgd', w_s, vg) + b_s.T[None, None, :, :, None]
    return u * vm.reshape(bsz, n, SGU_WIDTH)


def merge_branches(h, branches, w_gate, w_branch, w_out):
    acc = jax.nn.sigmoid(h @ w_gate[0]) * (branches[0] @ w_branch[0])
    for k in range(1, N_BRANCHES):
        acc = acc + jax.nn.sigmoid(h @ w_gate[k]) * (branches[k] @ w_branch[k])
    return acc @ w_out


def token_mixer(h_lat, h_ctx, rows, cols, layer_idx, need_ctx, w_in, conv_w, conv_b, a_log, dt_bias,
                ssd_d, ssd_norm_g, diff_lambda, diff_norm_g, pool_w, pool_scale, sgu_ln_g, sgu_ln_b,
                sgu_w, sgu_b, w_gate, w_branch, w_out):
    offsets = np.cumsum(PROJ_WIDTHS)[:-1].tolist()
    pool_l, z_l, xbc_l, dt_l, q_l, k_l, v_l, uv_l = jnp.split(h_lat @ w_in, offsets, axis=-1)
    pool_c, z_c, xbc_c, dt_c, q_c, k_c, v_c, uv_c = jnp.split(h_ctx @ w_in, offsets, axis=-1)

    y_pool_l = pool_mix(pool_l, pool_w, pool_scale)

    a_neg = -jnp.exp(a_log.astype(jnp.float32))
    xs_c, bm_c, cm_c, dtf_c, dtb_c = ssd_prep(xbc_c, dt_c, conv_w, conv_b, dt_bias)
    zero = jnp.zeros((h_ctx.shape[0], SSD_HEADS, SSD_HEADDIM, SSD_STATE), jnp.float32)
    yf_c, sf_c = ssd_chunked(xs_c, dtf_c, a_neg[0], bm_c, cm_c, zero, need_ctx)
    yb_c, sb_c = ssd_chunked(flip(xs_c), flip(dtb_c), a_neg[1], flip(bm_c), flip(cm_c), zero, need_ctx)
    xs_l, bm_l, cm_l, dtf_l, dtb_l = ssd_prep(xbc_l, dt_l, conv_w, conv_b, dt_bias)
    yf_l, _ = ssd_chunked(xs_l, dtf_l, a_neg[0], bm_l, cm_l, sf_c, True)
    yb_l, _ = ssd_chunked(flip(xs_l), flip(dtb_l), a_neg[1], flip(bm_l), flip(cm_l), sb_c, True)
    y_ssd_l = ssd_finish(yf_l, flip(yb_l), xs_l, z_l, ssd_d, ssd_norm_g)

    lam_init = 0.8 - 0.6 * math.exp(-0.3 * layer_idx)
    dl = diff_lambda.astype(jnp.float32)
    lam = jnp.exp(jnp.sum(dl[0] * dl[1])) - jnp.exp(jnp.sum(dl[2] * dl[3])) + lam_init
    q1_l, q2_l = diff_heads(q_l, rows, cols, True)
    k1_l, k2_l = diff_heads(k_l, rows, cols, True)
    k1_c, k2_c = diff_heads(k_c, rows, cols, False)
    vh_l, vh_c = value_heads(v_l), value_heads(v_c)
    k1_all = jnp.concatenate([k1_l, k1_c], axis=2)
    k2_all = jnp.concatenate([k2_l, k2_c], axis=2)
    v_all = jnp.concatenate([vh_l, vh_c], axis=2)
    y_diff_l = diff_head_norm(blocked_diff_attention(q1_l, q2_l, k1_all, k2_all, v_all, lam), diff_norm_g, lam_init)

    y_sgu_l = spatial_gating(uv_l, sgu_ln_g, sgu_ln_b, sgu_w, sgu_b)

    y_lat = merge_branches(h_lat, (y_pool_l, y_ssd_l, y_diff_l, y_sgu_l), w_gate, w_branch, w_out)
    if not need_ctx:
        return y_lat, None

    y_pool_c = pool_mix(pool_c, pool_w, pool_scale)
    y_ssd_c = ssd_finish(yf_c, flip(yb_c), xs_c, z_c, ssd_d, ssd_norm_g)
    q1_c, q2_c = diff_heads(q_c, rows, cols, False)
    y_diff_c = diff_head_norm(diff_attend(q1_c, q2_c, k1_c, k2_c, vh_c, lam), diff_norm_g, lam_init)
    y_sgu_c = spatial_gating(uv_c, sgu_ln_g, sgu_ln_b, sgu_w, sgu_b)
    y_ctx = merge_branches(h_ctx, (y_pool_c, y_ssd_c, y_diff_c, y_sgu_c), w_gate, w_branch, w_out)
    return y_lat, y_ctx


def expert_choice_ffn(h, w_router, w1, w3, w2):
    bsz, n, d = h.shape
    cap = EC_CAPACITY_FACTOR * n // N_EXPERTS
    aff = jax.nn.softmax((h @ w_router).astype(jnp.float32), axis=-1)
    gate, idx = lax.top_k(jnp.swapaxes(aff, 1, 2), cap)
    xe = jax.vmap(lambda hb, ib: hb[ib])(h, idx)
    hid = jax.nn.silu(jnp.einsum('becd,edf->becf', xe, w1)) * jnp.einsum('becd,edf->becf', xe, w3)
    ye = jnp.einsum('becf,efd->becd', hid, w2) * gate[..., None].astype(h.dtype)
    return jax.vmap(lambda ib, yb: jnp.zeros((n, d), yb.dtype).at[ib.reshape(-1)].add(yb.reshape(-1, d)))(idx, ye)


def setup_inputs(seed: int = 0) -> dict:
    key = jax.random.key(seed)
    keys = jax.random.split(key, 40)
    counter = [0]

    def nxt():
        k = keys[counter[0]]
        counter[0] += 1
        return k

    f32 = jnp.float32

    def nrm(shape, scale):
        return jax.random.normal(nxt(), shape, f32) * scale

    L, D = DEPTH, D_MODEL
    dt0 = jnp.exp(jax.random.uniform(nxt(), (L, 2, SSD_HEADS), f32, math.log(1e-3), math.log(1e-1)))
    return {
        'x': nrm((BATCH, SEQ, D), 1.0),
        'c': nrm((BATCH, D), 1.0),
        'ctx': nrm((BATCH, CTX_LEN, D), 1.0),
        'c_ctx': nrm((D,), 1.0),
        'w_mod': nrm((L, D, 6 * D), 0.5 * D ** -0.5),
        'b_mod': nrm((L, 6 * D), 0.01),
        'w_in': nrm((L, D, IN_COLS), D ** -0.5),
        'conv_w': nrm((L, SSD_CONV, SSD_XBC), SSD_CONV ** -0.5),
        'conv_b': nrm((L, SSD_XBC), 0.01),
        'a_log': jnp.log(jax.random.uniform(nxt(), (L, 2, SSD_HEADS), f32, 1.0, 16.0)),
        'dt_bias': dt0 + jnp.log(-jnp.expm1(-dt0)),
        'ssd_d': 1.0 + nrm((L, SSD_HEADS), 0.1),
        'ssd_norm_g': 1.0 + nrm((L, SSD_INNER), 0.02),
        'diff_lambda': nrm((L, 4, DIFF_DK), 0.1),
        'diff_norm_g': 1.0 + nrm((L, DIFF_DV), 0.02),
        'pool_w': nrm((L, len(POOL_WINDOWS), POOL_GROUP, POOL_GROUP), POOL_GROUP ** -0.5),
        'pool_scale': 1.0 + nrm((L, POOL_WIDTH), 0.1),
        'sgu_ln_g': 1.0 + nrm((L, SGU_WIDTH), 0.02),
        'sgu_ln_b': nrm((L, SGU_WIDTH), 0.01),
        'sgu_w': nrm((L, SGU_GROUPS, SGU_CHUNK, SGU_CHUNK), SGU_CHUNK ** -0.5),
        'sgu_b': 1.0 + nrm((L, SGU_GROUPS, SGU_CHUNK), 0.1),
        'w_gate': nrm((L, N_BRANCHES, D, D), D ** -0.5),
        'w_branch': nrm((L, N_BRANCHES, BRANCH_WIDTH, D), BRANCH_WIDTH ** -0.5 * BETA),
        'w_out': nrm((L, D, D), D ** -0.5 * BETA),
        'ln1_g': 1.0 + nrm((L, D), 0.02),
        'ln1_b': nrm((L, D), 0.01),
        'w_router': nrm((L, D, N_EXPERTS), D ** -0.5),
        'w1': nrm((L, N_EXPERTS, D, EXPERT_FF), D ** -0.5),
        'w3': nrm((L, N_EXPERTS, D, EXPERT_FF), D ** -0.5),
        'w2': nrm((L, N_EXPERTS, EXPERT_FF, D), EXPERT_FF ** -0.5 * BETA),
        'ln2_g': 1.0 + nrm((L, D), 0.02),
        'ln2_b': nrm((L, D), 0.01),
    }


def reference(x, c, ctx, c_ctx, w_mod, b_mod, w_in, conv_w, conv_b, a_log, dt_bias, ssd_d, ssd_norm_g,
              diff_lambda, diff_norm_g, pool_w, pool_scale, sgu_ln_g, sgu_ln_b, sgu_w, sgu_b, w_gate,
              w_branch, w_out, ln1_g, ln1_b, w_router, w1, w3, w2, ln2_g, ln2_b):
    n_lat = x.shape[1]
    n_rows = n_lat // GRID_W
    rows = jnp.broadcast_to(jnp.arange(n_rows, dtype=jnp.int32)[:, None], (n_rows, GRID_W)).reshape(-1)
    cols = jnp.broadcast_to(jnp.arange(GRID_W, dtype=jnp.int32)[None, :], (n_rows, GRID_W)).reshape(-1)
    for l in range(DEPTH):
        last = l == DEPTH - 1
        m_lat = jnp.split((jax.nn.silu(c) @ w_mod[l] + b_mod[l])[:, None, :], 6, axis=-1)
        m_ctx = jnp.split((jax.nn.silu(c_ctx) @ w_mod[l] + b_mod[l]).reshape(1, 1, -1), 6, axis=-1)
        h_lat = modulate(x, m_lat[0], m_lat[1])
        h_ctx = modulate(ctx, m_ctx[0], m_ctx[1])
        y_lat, y_ctx = token_mixer(h_lat, h_ctx, rows, cols, l, not last, w_in[l], conv_w[l], conv_b[l],
                                   a_log[l], dt_bias[l], ssd_d[l], ssd_norm_g[l], diff_lambda[l],
                                   diff_norm_g[l], pool_w[l], pool_scale[l], sgu_ln_g[l], sgu_ln_b[l],
                                   sgu_w[l], sgu_b[l], w_gate[l], w_branch[l], w_out[l])
        x = layer_norm(ALPHA * x + m_lat[2] * y_lat, ln1_g[l], ln1_b[l])
        y_ffn = expert_choice_ffn(modulate(x, m_lat[3], m_lat[4]), w_router[l], w1[l], w3[l], w2[l])
        x = layer_norm(ALPHA * x + m_lat[5] * y_ffn, ln2_g[l], ln2_b[l])
        if not last:
            ctx = layer_norm(ALPHA * ctx + m_ctx[2] * y_ctx, ln1_g[l], ln1_b[l])
            y_ffn_c = expert_choice_ffn(modulate(ctx, m_ctx[3], m_ctx[4]), w_router[l], w1[l], w3[l], w2[l])
            ctx = layer_norm(ALPHA * ctx + m_ctx[5] * y_ffn_c, ln2_g[l], ln2_b[l])
    return x
```

```python
import functools
import math

import jax
import jax.numpy as jnp
import numpy as np
from jax import lax
from jax.experimental import pallas as pl
from jax.experimental.pallas import tpu as pltpu

F32 = jnp.float32
BF16 = jnp.bfloat16

D_MODEL = 1024
DEPTH = 2
GRID_W = 64
N_BRANCHES = 4
BRANCH_WIDTH = 512
POOL_WIDTH = 512
POOL_GROUP = 128
POOL_WINDOWS = (2, 4, 8, 16)
SSD_HEADS = 8
SSD_HEADDIM = 64
SSD_INNER = SSD_HEADS * SSD_HEADDIM
SSD_GROUPS = 2
SSD_STATE = 64
SSD_XBC = SSD_INNER + 2 * SSD_GROUPS * SSD_STATE
SSD_CONV = 5
SSD_CHUNK = 128
DIFF_HEADS = 4
DIFF_DK = 64
DIFF_DV = 2 * DIFF_DK
DIFF_QK_WIDTH = DIFF_HEADS * 2 * DIFF_DK
DIFF_V_WIDTH = DIFF_HEADS * DIFF_DV
ROPE_BASE = 10000.0
SGU_WIDTH = 512
SGU_GROUPS = 4
SGU_GROUP_W = SGU_WIDTH // SGU_GROUPS
SGU_CHUNK = 128
N_EXPERTS = 16
EXPERT_FF = 512
EC_CAPACITY_FACTOR = 2
LN_EPS = 1e-5
ALPHA = (2 * DEPTH) ** 0.25
PROJ_WIDTHS = (POOL_WIDTH, SSD_INNER, SSD_XBC, 2 * SSD_HEADS, DIFF_QK_WIDTH, DIFF_QK_WIDTH, DIFF_V_WIDTH,
               2 * SGU_WIDTH)
MAIN_NAMES = ("pool", "z", "xbc", "q", "k", "v", "uv")
MAIN_WIDTHS = (POOL_WIDTH, SSD_INNER, SSD_XBC, DIFF_QK_WIDTH, DIFF_QK_WIDTH, DIFF_V_WIDTH, 2 * SGU_WIDTH)

LANES = 128
VMEM_LIMIT = 56 * 1024 * 1024


def _cparams(*sem):
    return pltpu.CompilerParams(dimension_semantics=sem, vmem_limit_bytes=VMEM_LIMIT)


def _resident(shape):
    nd = len(shape)
    return pl.BlockSpec(shape, lambda *_: (0,) * nd, pipeline_mode=pl.Buffered(1))


def _mod_kernel(c_ref, w_ref, b_ref, o_ref):
    c = c_ref[...]
    s = c * jax.nn.sigmoid(c)
    o_ref[...] = jnp.dot(s, w_ref[...], precision=lax.Precision.HIGHEST, preferred_element_type=F32) + b_ref[...]


def mod_vectors(cc, w_mod_l, b_mod_l):
    r, d = cc.shape
    n = w_mod_l.shape[1]
    tn = 512
    return pl.pallas_call(
        _mod_kernel,
        grid=(n // tn,),
        in_specs=[pl.BlockSpec((r, d), lambda j: (0, 0)),
                  pl.BlockSpec((d, tn), lambda j: (0, j)),
                  pl.BlockSpec((1, tn), lambda j: (0, j))],
        out_specs=pl.BlockSpec((r, tn), lambda j: (0, j)),
        out_shape=jax.ShapeDtypeStruct((r, n), F32),
        compiler_params=_cparams("arbitrary"),
    )(cc, w_mod_l, b_mod_l.reshape(1, n))


def _rope(t, cos, sin_signed, lower):
    partner = jnp.where(lower, pltpu.roll(t, LANES - 16, 1), pltpu.roll(t, 16, 1))
    return t * cos + partner * sin_signed


def _inproj_kernel(x_ref, shift_ref, scale_ref, w_ref, wdt_ref, wdtT_ref, cos_ref, sin_ref,
                   pool_o, z_o, xbc_o, q_o, k_o, v_o, uv_o, dt_o, dtT_o, *, rope):
    h = x_ref[...] * (1.0 + scale_ref[...]) + shift_ref[...]
    hb = h.astype(BF16)
    outs = dict(pool=pool_o, z=z_o, xbc=xbc_o, q=q_o, k=k_o, v=v_o, uv=uv_o)
    if rope:
        cos = cos_ref[...]
        sin = sin_ref[...]
        lane = lax.broadcasted_iota(jnp.int32, cos.shape, 1)
        lower = (lane % 32) < 16
    off = 0
    for name, width in zip(MAIN_NAMES, MAIN_WIDTHS):
        o_ref = outs[name]
        if rope and name in ("q", "k"):
            for j in range(width // LANES):
                r = jnp.dot(hb, w_ref[:, off + j * LANES:off + (j + 1) * LANES], preferred_element_type=F32)
                o_ref[:, j * LANES:(j + 1) * LANES] = _rope(r, cos, sin, lower)
        else:
            o_ref[...] = jnp.dot(hb, w_ref[:, off:off + width], preferred_element_type=F32)
        off += width
    dt_o[...] = jnp.dot(hb, wdt_ref[...], preferred_element_type=F32)
    dtT_o[...] = lax.dot_general(wdtT_ref[...], hb, (((1,), (1,)), ((), ())), preferred_element_type=F32)


def in_projection(x, shift, scale, w_main, w_dt, w_dtT, cos_t, sin_t, rope, tm):
    bsz, n, d = x.shape
    nt = n // tm
    row = lambda w: pl.BlockSpec((None, tm, w), lambda b, i: (b, i, 0))
    vec = pl.BlockSpec((None, 1, d), lambda b, i: (b, 0, 0))
    ndt = w_dt.shape[1]
    out_shapes = [jax.ShapeDtypeStruct((bsz, n, w), F32) for w in MAIN_WIDTHS]
    out_shapes += [jax.ShapeDtypeStruct((bsz, n, ndt), F32), jax.ShapeDtypeStruct((bsz, ndt, n), F32)]
    out_specs = [row(w) for w in MAIN_WIDTHS]
    out_specs += [row(ndt), pl.BlockSpec((None, ndt, tm), lambda b, i: (b, 0, i))]
    res = pl.pallas_call(
        functools.partial(_inproj_kernel, rope=rope),
        grid=(bsz, nt),
        in_specs=[row(d), vec, vec, _resident(w_main.shape), _resident(w_dt.shape), _resident(w_dtT.shape),
                  pl.BlockSpec((tm, LANES), lambda b, i: (i, 0)),
                  pl.BlockSpec((tm, LANES), lambda b, i: (i, 0))],
        out_specs=out_specs,
        out_shape=out_shapes,
        compiler_params=_cparams("parallel", "parallel"),
    )(x, shift, scale, w_main, w_dt, w_dtT, cos_t, sin_t)
    out = dict(zip(MAIN_NAMES, res[:len(MAIN_NAMES)]))
    out["dt"] = res[-2]
    out["dtT"] = res[-1]
    return out


def rope_tables(n):
    half = DIFF_DK // 4
    t = jnp.arange(n, dtype=jnp.int32)
    rows = (t // GRID_W).astype(F32)
    cols = (t % GRID_W).astype(F32)
    inv = ROPE_BASE ** (-jnp.arange(half, dtype=F32) / half)
    ang_r = rows[:, None] * inv
    ang_c = cols[:, None] * inv
    cos64 = jnp.concatenate([jnp.cos(ang_r)] * 2 + [jnp.cos(ang_c)] * 2, axis=-1)
    sin64 = jnp.concatenate([-jnp.sin(ang_r), jnp.sin(ang_r), -jnp.sin(ang_c), jnp.sin(ang_c)], axis=-1)
    return jnp.concatenate([cos64, cos64], axis=-1), jnp.concatenate([sin64, sin64], axis=-1)


def _softmax_rows(parts):
    m = parts[0].max(axis=-1, keepdims=True)
    for p in parts[1:]:
        m = jnp.maximum(m, p.max(axis=-1, keepdims=True))
    es = [jnp.exp(p - m) for p in parts]
    tot = es[0].sum(axis=-1, keepdims=True)
    for e in es[1:]:
        tot = tot + e.sum(axis=-1, keepdims=True)
    inv = 1.0 / tot
    return [e * inv for e in es]


def _diff_attn_kernel(lam_ref, q_ref, g_ref, *refs, nseg, lam_init):
    k_refs = refs[:nseg]
    v_refs = refs[nseg:2 * nseg]
    o_ref = refs[2 * nseg]
    lam = lam_ref[0]
    scale = DIFF_DK ** -0.5
    g = g_ref[...]
    for hd in range(DIFF_HEADS):
        base = hd * DIFF_DV
        q1 = q_ref[:, base:base + DIFF_DK].astype(BF16)
        q2 = q_ref[:, base + DIFF_DK:base + DIFF_DV].astype(BF16)
        s1, s2 = [], []
        for kr in k_refs:
            k1 = kr[:, base:base + DIFF_DK].astype(BF16)
            k2 = kr[:, base + DIFF_DK:base + DIFF_DV].astype(BF16)
            s1.append(lax.dot_general(q1, k1, (((1,), (1,)), ((), ())), preferred_element_type=F32) * scale)
            s2.append(lax.dot_general(q2, k2, (((1,), (1,)), ((), ())), preferred_element_type=F32) * scale)
        a1 = _softmax_rows(s1)
        a2 = _softmax_rows(s2)
        o = None
        for p1, p2, vr in zip(a1, a2, v_refs):
            a = (p1 - lam * p2).astype(BF16)
            c = jnp.dot(a, vr[:, base:base + DIFF_DV].astype(BF16), preferred_element_type=F32)
            o = c if o is None else o + c
        o = o * lax.rsqrt(jnp.mean(jnp.square(o), axis=-1, keepdims=True) + LN_EPS) * g * (1.0 - lam_init)
        o_ref[:, base:base + DIFF_DV] = o


def diff_attention(lam, q, ks, vs, g, lam_init, tq):
    bsz, n, w = q.shape
    nseg = len(ks)
    seg = lambda m: pl.BlockSpec((None, m, w), lambda b, i: (b, 0, 0))
    return pl.pallas_call(
        functools.partial(_diff_attn_kernel, nseg=nseg, lam_init=lam_init),
        grid=(bsz, n // tq),
        in_specs=[pl.BlockSpec(memory_space=pltpu.SMEM),
                  pl.BlockSpec((None, tq, w), lambda b, i: (b, i, 0)),
                  pl.BlockSpec((1, DIFF_DV), lambda b, i: (0, 0))]
                 + [seg(k.shape[1]) for k in ks] + [seg(v.shape[1]) for v in vs],
        out_specs=pl.BlockSpec((None, tq, w), lambda b, i: (b, i, 0)),
        out_shape=jax.ShapeDtypeStruct((bsz, n, w), F32),
        compiler_params=_cparams("parallel", "parallel"),
    )(lam, q, g.reshape(1, DIFF_DV), *ks, *vs)


def _layer_norm(v, g, b):
    mu = jnp.mean(v, axis=-1, keepdims=True)
    var = jnp.mean(jnp.square(v - mu), axis=-1, keepdims=True)
    return (v - mu) * lax.rsqrt(var + LN_EPS) * g + b


def _merge_kernel(x_ref, shift_ref, scale_ref, gate_ref, b0, b1, b2, b3, wg_ref, wb_ref, wo_ref, lg_ref, lb_ref,
                  o_ref):
    x = x_ref[...]
    hb = (x * (1.0 + scale_ref[...]) + shift_ref[...]).astype(BF16)
    acc = None
    for k, br in enumerate((b0, b1, b2, b3)):
        gk = jax.nn.sigmoid(jnp.dot(hb, wg_ref[k], preferred_element_type=F32))
        yk = jnp.dot(br[...].astype(BF16), wb_ref[k], preferred_element_type=F32)
        acc = gk * yk if acc is None else acc + gk * yk
    y = jnp.dot(acc.astype(BF16), wo_ref[...], preferred_element_type=F32)
    o_ref[...] = _layer_norm(ALPHA * x + gate_ref[...] * y, lg_ref[...], lb_ref[...])


def merge_residual_ln(x, shift, scale, gate, branches, wg, wb, wo, ln_g, ln_b, tm):
    bsz, n, d = x.shape
    row = lambda w: pl.BlockSpec((None, tm, w), lambda b, i: (b, i, 0))
    vec = pl.BlockSpec((None, 1, d), lambda b, i: (b, 0, 0))
    return pl.pallas_call(
        _merge_kernel,
        grid=(bsz, n // tm),
        in_specs=[row(d), vec, vec, vec] + [row(BRANCH_WIDTH)] * N_BRANCHES
                 + [_resident(wg.shape), _resident(wb.shape), _resident(wo.shape),
                    _resident((1, d)), _resident((1, d))],
        out_specs=row(d),
        out_shape=jax.ShapeDtypeStruct((bsz, n, d), F32),
        compiler_params=_cparams("parallel", "parallel"),
    )(x, shift, scale, gate, *branches, wg, wb, wo, ln_g.reshape(1, d), ln_b.reshape(1, d))


def _res_ln_kernel(x_ref, y_ref, gate_ref, lg_ref, lb_ref, o_ref):
    o_ref[...] = _layer_norm(ALPHA * x_ref[...] + gate_ref[...] * y_ref[...], lg_ref[...], lb_ref[...])


def residual_ln(x, y, gate, ln_g, ln_b, tm):
    bsz, n, d = x.shape
    row = pl.BlockSpec((None, tm, d), lambda b, i: (b, i, 0))
    vec = pl.BlockSpec((None, 1, d), lambda b, i: (b, 0, 0))
    return pl.pallas_call(
        _res_ln_kernel,
        grid=(bsz, n // tm),
        in_specs=[row, row, vec, _resident((1, d)), _resident((1, d))],
        out_specs=row,
        out_shape=jax.ShapeDtypeStruct((bsz, n, d), F32),
        compiler_params=_cparams("parallel", "parallel"),
    )(x, y, gate, ln_g.reshape(1, d), ln_b.reshape(1, d))


def _expert_kernel(xe_ref, gate_ref, w1_ref, w3_ref, w2_ref, o_ref):
    sb, cap, d = xe_ref.shape
    xe = xe_ref[...].reshape(sb * cap, d).astype(BF16)
    a = jnp.dot(xe, w1_ref[...], preferred_element_type=F32)
    b = jnp.dot(xe, w3_ref[...], preferred_element_type=F32)
    hid = (a * jax.nn.sigmoid(a) * b).astype(BF16)
    ye = jnp.dot(hid, w2_ref[...], preferred_element_type=F32)
    o_ref[...] = ye.reshape(sb, cap, d) * gate_ref[...]


def expert_ffn(xe, gate, w1, w3, w2, sb):
    bsz, ne, cap, d = xe.shape
    ff = w1.shape[-1]
    return pl.pallas_call(
        _expert_kernel,
        grid=(ne, bsz // sb),
        in_specs=[pl.BlockSpec((sb, None, cap, d), lambda e, b: (b, e, 0, 0)),
                  pl.BlockSpec((sb, None, cap, 1), lambda e, b: (b, e, 0, 0)),
                  pl.BlockSpec((None, d, ff), lambda e, b: (e, 0, 0)),
                  pl.BlockSpec((None, d, ff), lambda e, b: (e, 0, 0)),
                  pl.BlockSpec((None, ff, d), lambda e, b: (e, 0, 0))],
        out_specs=pl.BlockSpec((sb, None, cap, d), lambda e, b: (b, e, 0, 0)),
        out_shape=jax.ShapeDtypeStruct((bsz, ne, cap, d), F32),
        compiler_params=_cparams("parallel", "parallel"),
    )(xe, gate, w1, w3, w2)


def _jx_layer_norm(x, g, b):
    mu = jnp.mean(x, axis=-1, keepdims=True)
    var = jnp.mean(jnp.square(x - mu), axis=-1, keepdims=True)
    return (x - mu) * lax.rsqrt(var + LN_EPS) * g + b


def _jx_pool_mix(xp, pool_w, pool_scale):
    bsz, n, _ = xp.shape
    cs = jnp.concatenate([jnp.zeros((bsz, 1, POOL_WIDTH), F32), jnp.cumsum(xp, axis=1)], axis=1)
    t = jnp.arange(n)
    outs = []
    for g, w in enumerate(POOL_WINDOWS):
        lo = jnp.clip(t - w // 2, 0, n)
        hi = jnp.clip(t + w // 2, 0, n)
        csg = cs[:, :, g * POOL_GROUP:(g + 1) * POOL_GROUP]
        mean = (csg[:, hi] - csg[:, lo]) / (hi - lo).astype(F32)[None, :, None]
        outs.append(mean - xp[:, :, g * POOL_GROUP:(g + 1) * POOL_GROUP])
    pooled = jnp.stack(outs, axis=2)
    mixed = jnp.einsum('bngc,gcd->bngd', pooled, pool_w).reshape(bsz, n, POOL_WIDTH)
    return mixed * pool_scale


def _jx_depthwise_conv(x, w, b):
    ch = x.shape[-1]
    y = lax.conv_general_dilated(x, w[:, None, :], window_strides=(1,),
                                 padding=[(SSD_CONV // 2, SSD_CONV // 2)],
                                 dimension_numbers=('NWC', 'WIO', 'NWC'), feature_group_count=ch)
    return y + b


def _jx_ssd_prep(xbc, dt_raw, conv_w, conv_b, dt_bias):
    bsz, n, _ = xbc.shape
    xbc = jax.nn.silu(_jx_depthwise_conv(xbc, conv_w, conv_b))
    xs, bm, cm = jnp.split(xbc, [SSD_INNER, SSD_INNER + SSD_GROUPS * SSD_STATE], axis=-1)
    dt = jax.nn.softplus(dt_raw.reshape(bsz, n, 2, SSD_HEADS) + dt_bias)
    return (xs.reshape(bsz, n, SSD_HEADS, SSD_HEADDIM), bm.reshape(bsz, n, SSD_GROUPS, SSD_STATE),
            cm.reshape(bsz, n, SSD_GROUPS, SSD_STATE), dt[:, :, 0], dt[:, :, 1])


def _jx_ssd_chunked(x, dt, a_neg, bmat, cmat, init_state, with_output):
    bsz, n, nh, hp = x.shape
    nc = n // SSD_CHUNK
    rep = nh // SSD_GROUPS
    xdt = (x * dt[..., None]).reshape(bsz, nc, SSD_CHUNK, nh, hp)
    a_cs = jnp.cumsum((dt * a_neg).reshape(bsz, nc, SSD_CHUNK, nh).transpose(0, 3, 1, 2), axis=-1)
    bh = jnp.repeat(bmat, rep, axis=2).reshape(bsz, nc, SSD_CHUNK, nh, SSD_STATE)
    ch = jnp.repeat(cmat, rep, axis=2).reshape(bsz, nc, SSD_CHUNK, nh, SSD_STATE)
    decay_to_end = jnp.exp(a_cs[..., -1:] - a_cs)
    chunk_states = jnp.einsum('bclhn,bhcl,bclhp->bchpn', bh, decay_to_end, xdt)
    chunk_decay = jnp.exp(a_cs[..., -1])

    def carry_step(state, inp):
        s_k, d_k = inp
        return state * d_k[:, :, None, None] + s_k, state

    final, entering = lax.scan(carry_step, init_state,
                               (jnp.moveaxis(chunk_states, 1, 0), jnp.moveaxis(chunk_decay, 2, 0)))
    if not with_output:
        return None, final
    entering = jnp.moveaxis(entering, 0, 1)
    seg = a_cs[..., :, None] - a_cs[..., None, :]
    lower = jnp.tril(jnp.ones((SSD_CHUNK, SSD_CHUNK), dtype=bool))
    lmat = jnp.exp(jnp.where(lower, seg, -jnp.inf))
    scores = jnp.einsum('bclhn,bcshn->bhcls', ch, bh) * lmat
    y = (jnp.einsum('bhcls,bcshp->bclhp', scores, xdt)
         + jnp.einsum('bclhn,bchpn,bhcl->bclhp', ch, entering, jnp.exp(a_cs)))
    return y.reshape(bsz, n, nh, hp), final


def _jx_ssd_finish(y_f, y_b, xs, z, d_skip, g):
    bsz, n = z.shape[:2]
    y = y_f + y_b + d_skip[:, None] * xs
    y = y.reshape(bsz, n, SSD_INNER) * jax.nn.silu(z)
    return y * lax.rsqrt(jnp.mean(jnp.square(y), axis=-1, keepdims=True) + LN_EPS) * g


def _jx_spatial_gating(uv, ln_g, ln_b, w_s, b_s):
    bsz, n, _ = uv.shape
    u, v = jnp.split(jax.nn.gelu(uv), 2, axis=-1)
    v = _jx_layer_norm(v, ln_g, ln_b)
    vg = v.reshape(bsz, n // SGU_CHUNK, SGU_CHUNK, SGU_GROUPS, SGU_GROUP_W)
    vm = jnp.einsum('gpq,bcqgd->bcpgd', w_s, vg) + b_s.T[None, None, :, :, None]
    return u * vm.reshape(bsz, n, SGU_WIDTH)


def _flip(t):
    return jnp.flip(t, axis=1)


def _expert_choice(h, w_router, w1, w3, w2, sb):
    bsz, n, d = h.shape
    cap = EC_CAPACITY_FACTOR * n // N_EXPERTS
    aff = jax.nn.softmax(h @ w_router, axis=-1)
    gate, idx = lax.top_k(jnp.swapaxes(aff, 1, 2), cap)
    xe = jax.vmap(lambda hb, ib: hb[ib])(h, idx)
    ye = expert_ffn(xe, gate[..., None], w1, w3, w2, sb)
    return jax.vmap(lambda ib, yb: jnp.zeros((n, d), yb.dtype).at[ib.reshape(-1)].add(yb.reshape(-1, d)))(idx, ye)


def _split_w_in(w_in_l):
    offs = np.cumsum((0,) + PROJ_WIDTHS)
    parts = [w_in_l[:, offs[i]:offs[i + 1]] for i in range(len(PROJ_WIDTHS))]
    pool, z, xbc, dt, q, k, v, uv = parts
    w_main = jnp.concatenate([pool, z, xbc, q, k, v, uv], axis=1).astype(BF16)
    return w_main, dt.astype(BF16), dt.T.astype(BF16)


def kernel(x, c, ctx, c_ctx, w_mod, b_mod, w_in, conv_w, conv_b, a_log, dt_bias, ssd_d, ssd_norm_g, diff_lambda,
           diff_norm_g, pool_w, pool_scale, sgu_ln_g, sgu_ln_b, sgu_w, sgu_b, w_gate, w_branch, w_out, ln1_g, ln1_b,
           w_router, w1, w3, w2, ln2_g, ln2_b):
    bsz, n_lat, d = x.shape
    n_ctx = ctx.shape[1]
    depth = w_in.shape[0]
    cos_t, sin_t = rope_tables(n_lat)
    cc = jnp.concatenate([c, c_ctx[None, :], jnp.zeros((7, d), F32)], axis=0)
    tm_lat = 512
    for l in range(depth):
        last = l == depth - 1
        mods = mod_vectors(cc, w_mod[l], b_mod[l])
        m_lat = [m[:, None, :] for m in jnp.split(mods[:bsz], 6, axis=-1)]
        m_ctx = [jnp.broadcast_to(m[None], (bsz, 1, d)) for m in jnp.split(mods[bsz:bsz + 1], 6, axis=-1)]
        w_main, w_dt, w_dtT = _split_w_in(w_in[l])
        pl_ = in_projection(x, m_lat[0], m_lat[1], w_main, w_dt, w_dtT, cos_t, sin_t, True, tm_lat)
        pc_ = in_projection(ctx, m_ctx[0], m_ctx[1], w_main, w_dt, w_dtT, cos_t, sin_t, False, n_ctx)

        y_pool_l = _jx_pool_mix(pl_["pool"], pool_w[l], pool_scale[l])

        a_neg = -jnp.exp(a_log[l])
        xs_c, bm_c, cm_c, dtf_c, dtb_c = _jx_ssd_prep(pc_["xbc"], pc_["dt"], conv_w[l], conv_b[l], dt_bias[l])
        zero = jnp.zeros((bsz, SSD_HEADS, SSD_HEADDIM, SSD_STATE), F32)
        yf_c, sf_c = _jx_ssd_chunked(xs_c, dtf_c, a_neg[0], bm_c, cm_c, zero, not last)
        yb_c, sb_c = _jx_ssd_chunked(_flip(xs_c), _flip(dtb_c), a_neg[1], _flip(bm_c), _flip(cm_c), zero, not last)
        xs_l, bm_l, cm_l, dtf_l, dtb_l = _jx_ssd_prep(pl_["xbc"], pl_["dt"], conv_w[l], conv_b[l], dt_bias[l])
        yf_l, _ = _jx_ssd_chunked(xs_l, dtf_l, a_neg[0], bm_l, cm_l, sf_c, True)
        yb_l, _ = _jx_ssd_chunked(_flip(xs_l), _flip(dtb_l), a_neg[1], _flip(bm_l), _flip(cm_l), sb_c, True)
        y_ssd_l = _jx_ssd_finish(yf_l, _flip(yb_l), xs_l, pl_["z"], ssd_d[l], ssd_norm_g[l])

        lam_init = 0.8 - 0.6 * math.exp(-0.3 * l)
        dl = diff_lambda[l]
        lam = (jnp.exp(jnp.sum(dl[0] * dl[1])) - jnp.exp(jnp.sum(dl[2] * dl[3])) + lam_init).reshape(1)
        y_diff_l = diff_attention(lam, pl_["q"], [pl_["k"], pc_["k"]], [pl_["v"], pc_["v"]], diff_norm_g[l],
                                  lam_init, 256)

        y_sgu_l = _jx_spatial_gating(pl_["uv"], sgu_ln_g[l], sgu_ln_b[l], sgu_w[l], sgu_b[l])

        wg = w_gate[l].astype(BF16)
        wb = w_branch[l].astype(BF16)
        wo = w_out[l].astype(BF16)
        x1 = merge_residual_ln(x, m_lat[0], m_lat[1], m_lat[2], (y_pool_l, y_ssd_l, y_diff_l, y_sgu_l),
                               wg, wb, wo, ln1_g[l], ln1_b[l], tm_lat)
        w1b, w3b, w2b = w1[l].astype(BF16), w3[l].astype(BF16), w2[l].astype(BF16)
        y_ffn = _expert_choice(x1 * (1.0 + m_lat[4]) + m_lat[3], w_router[l], w1b, w3b, w2b, 4)
        x = residual_ln(x1, y_ffn, m_lat[5], ln2_g[l], ln2_b[l], tm_lat)

        if not last:
            y_pool_c = _jx_pool_mix(pc_["pool"], pool_w[l], pool_scale[l])
            y_ssd_c = _jx_ssd_finish(yf_c, _flip(yb_c), xs_c, pc_["z"], ssd_d[l], ssd_norm_g[l])
            y_diff_c = diff_attention(lam, pc_["q"], [pc_["k"]], [pc_["v"]], diff_norm_g[l], lam_init, n_ctx)
            y_sgu_c = _jx_spatial_gating(pc_["uv"], sgu_ln_g[l], sgu_ln_b[l], sgu_w[l], sgu_b[l])
            c1 = merge_residual_ln(ctx, m_ctx[0], m_ctx[1], m_ctx[2], (y_pool_c, y_ssd_c, y_diff_c, y_sgu_c),
                                   wg, wb, wo, ln1_g[l], ln1_b[l], n_ctx)
            y_ffn_c = _expert_choice(c1 * (1.0 + m_ctx[4]) + m_ctx[3], w_router[l], w1b, w3b, w2b, bsz)
            ctx = residual_ln(c1, y_ffn_c, m_ctx[5], ln2_g[l], ln2_b[l], n_ctx)
    return x
```
